```python
import jax, jax.numpy as jnp
from jax import lax
import numpy as np

D_MODEL = 1024
BATCH = 2
SEQ = 16384
DEPTH = 1
DEC_BATCH = 8
DEC_SEQ = 16
PAST_LEN = 4096

CHUNK = 64
BAND_PAST_CHUNKS = 8
BAND_CHUNKS = BAND_PAST_CHUNKS + 1
HEAD_DIM = 64
H_A = 8
H_B = 8
W_A = H_A * HEAD_DIM
W_B = H_B * HEAD_DIM
REL_CLIP_PAST = 128
REL_TABLE = REL_CLIP_PAST + CHUNK
Q_BLOCK = 128
EPS = 1e-6
NEG_INF = -1e30
FORGET_BIAS_MEAN = 2.0
SCALE = HEAD_DIM ** -0.5
SPLIT_SIZES = [W_A] * 4 + [W_B] * 4 + [H_B, D_MODEL, D_MODEL]
N_IN = sum(SPLIT_SIZES)

kernel_name = "chunk_band_fox_hybrid_step"


def rmsnorm(x, g):
    x32 = x.astype(jnp.float32)
    y = x32 * lax.rsqrt(jnp.mean(x32 * x32, axis=-1, keepdims=True) + EPS) * g.astype(jnp.float32)
    return y.astype(x.dtype)


def rel_bias(table, dist):
    idx = jnp.clip(dist, -(CHUNK - 1), REL_CLIP_PAST) + (CHUNK - 1)
    return jnp.take(table, idx, axis=1).astype(jnp.float32)


def mixer_inputs(x, g_pre, w_in, b_f):
    h = rmsnorm(x, g_pre)
    z = jnp.einsum('bsd,dn->bsn', h, w_in)
    cuts = np.cumsum(SPLIT_SIZES)[:-1].tolist()
    qa, ka, va, za, qb, kb, vb, zb, fl, ma, mb = jnp.split(z, cuts, axis=-1)
    b, s = x.shape[0], x.shape[1]
    heads = lambda t, nh: t.reshape(b, s, nh, HEAD_DIM)
    logf = jax.nn.log_sigmoid((fl + b_f).astype(jnp.float32))
    return (heads(qa, H_A), heads(ka, H_A), heads(va, H_A), za,
            heads(qb, H_B), heads(kb, H_B), heads(vb, H_B), zb, logf, ma, mb)


def mixer_outputs(x, ya, yb, za, zb, ma, mb, w_br_a, w_br_b, w_out, g_post):
    ua = jnp.einsum('bsw,wd->bsd', ya.astype(x.dtype) * jax.nn.silu(za), w_br_a)
    ub = jnp.einsum('bsw,wd->bsd', yb.astype(x.dtype) * jax.nn.silu(zb), w_br_b)
    merged = jax.nn.sigmoid(ma) * ua + jax.nn.sigmoid(mb) * ub
    out = jnp.einsum('bsd,de->bse', merged, w_out)
    return x + rmsnorm(out, g_post)


def chunk_band_attention_prompt(q, k, v, table):
    b, s, h, d = q.shape
    nc = s // CHUNK
    pad = BAND_PAST_CHUNKS * CHUNK
    band = BAND_CHUNKS * CHUNK
    kp = jnp.pad(k, ((0, 0), (pad, 0), (0, 0), (0, 0)))
    vp = jnp.pad(v, ((0, 0), (pad, 0), (0, 0), (0, 0)))
    qc = jnp.moveaxis(q.reshape(b, nc, CHUNK, h, d), 1, 0)
    ki = jnp.arange(band)
    qi = jnp.arange(CHUNK)
    bias = rel_bias(table, (qi[:, None] + pad) - ki[None, :])

    def one_chunk(args):
        c, q_c = args
        k_c = lax.dynamic_slice_in_dim(kp, c * CHUNK, band, axis=1)
        v_c = lax.dynamic_slice_in_dim(vp, c * CHUNK, band, axis=1)
        sc = jnp.einsum('bqhd,bkhd->bhqk', q_c, k_c).astype(jnp.float32) * SCALE + bias[None]
        valid = (c * CHUNK - pad + ki) >= 0
        sc = jnp.where(valid[None, None, None, :], sc, NEG_INF)
        p = jax.nn.softmax(sc, axis=-1).astype(v.dtype)
        return jnp.einsum('bhqk,bkhd->bqhd', p, v_c)

    out = lax.map(one_chunk, (jnp.arange(nc), qc))
    return jnp.moveaxis(out, 0, 1).reshape(b, s, h * d)


def chunk_band_attention_sample(q, k_all, v_all, n_cache, table):
    b, t, h, d = q.shape
    sc = jnp.einsum('bqhd,bkhd->bhqk', q, k_all).astype(jnp.float32) * SCALE
    q_pos = n_cache + jnp.arange(t)
    k_pos = jnp.arange(n_cache + t)
    sc = sc + rel_bias(table, q_pos[:, None] - k_pos[None, :])[None]
    p = jax.nn.softmax(sc, axis=-1).astype(v_all.dtype)
    return jnp.einsum('bhqk,bkhd->bqhd', p, v_all).reshape(b, t, h * d)


def fox_prompt(q, k, v, logf):
    b, s, h, d = q.shape
    nb = s // Q_BLOCK
    F = jnp.cumsum(logf, axis=1)
    F_k = jnp.transpose(F, (0, 2, 1))
    qb = jnp.moveaxis(q.reshape(b, nb, Q_BLOCK, h, d), 1, 0)
    Fq = jnp.moveaxis(F.reshape(b, nb, Q_BLOCK, h), 1, 0)
    k_pos = jnp.arange(s)

    def one_block(args):
        i, q_i, F_i = args
        sc = jnp.einsum('bqhd,bkhd->bhqk', q_i, k).astype(jnp.float32) * SCALE
        sc = sc + jnp.transpose(F_i, (0, 2, 1))[..., None] - F_k[:, :, None, :]
        q_pos = i * Q_BLOCK + jnp.arange(Q_BLOCK)
        sc = jnp.where((k_pos[None, :] <= q_pos[:, None])[None, None], sc, NEG_INF)
        p = jax.nn.softmax(sc, axis=-1).astype(v.dtype)
        return jnp.einsum('bhqk,bkhd->bqhd', p, v)

    out = lax.map(one_block, (jnp.arange(nb), qb, Fq))
    return jnp.moveaxis(out, 0, 1).reshape(b, s, h * d)


def fox_sample(q, k_all, v_all, logf_all, n_cache):
    b, t, h, d = q.shape
    F = jnp.cumsum(logf_all.astype(jnp.float32), axis=1)
    F_k = jnp.transpose(F, (0, 2, 1))
    F_q = F_k[:, :, n_cache:]
    sc = jnp.einsum('bqhd,bkhd->bhqk', q, k_all).astype(jnp.float32) * SCALE
    sc = sc + F_q[..., None] - F_k[:, :, None, :]
    q_pos = n_cache + jnp.arange(t)
    k_pos = jnp.arange(n_cache + t)
    sc = jnp.where((k_pos[None, :] <= q_pos[:, None])[None, None], sc, NEG_INF)
    p = jax.nn.softmax(sc, axis=-1).astype(v_all.dtype)
    return jnp.einsum('bhqk,bkhd->bqhd', p, v_all).reshape(b, t, h * d)


def setup_inputs(seed: int = 0) -> dict:
    key = jax.random.key(seed)
    ks = jax.random.split(key, 16)
    a_keep = min(BAND_PAST_CHUNKS * CHUNK, PAST_LEN)
    nrm = lambda k, shape, s=1.0: s * jax.random.normal(k, shape, jnp.float32)
    return {
        "x_prompt": nrm(ks[0], (BATCH, SEQ, D_MODEL)),
        "x_sample": nrm(ks[1], (DEC_BATCH, DEC_SEQ, D_MODEL)),
        "cache_a_k": nrm(ks[2], (DEPTH, DEC_BATCH, a_keep, H_A, HEAD_DIM)),
        "cache_a_v": nrm(ks[3], (DEPTH, DEC_BATCH, a_keep, H_A, HEAD_DIM)),
        "cache_b_k": nrm(ks[4], (DEPTH, DEC_BATCH, PAST_LEN, H_B, HEAD_DIM)),
        "cache_b_v": nrm(ks[5], (DEPTH, DEC_BATCH, PAST_LEN, H_B, HEAD_DIM)),
        "cache_b_logf": jax.nn.log_sigmoid(FORGET_BIAS_MEAN + nrm(ks[6], (DEPTH, DEC_BATCH, PAST_LEN, H_B))),
        "g_pre": 1.0 + nrm(ks[7], (DEPTH, D_MODEL), 0.05),
        "w_in": nrm(ks[8], (DEPTH, D_MODEL, N_IN), D_MODEL ** -0.5),
        "b_f": FORGET_BIAS_MEAN + nrm(ks[9], (DEPTH, H_B), 0.5),
        "rel_table": nrm(ks[10], (DEPTH, H_A, REL_TABLE), 0.5),
        "w_br_a": nrm(ks[11], (DEPTH, W_A, D_MODEL), W_A ** -0.5),
        "w_br_b": nrm(ks[12], (DEPTH, W_B, D_MODEL), W_B ** -0.5),
        "w_out": nrm(ks[13], (DEPTH, D_MODEL, D_MODEL), D_MODEL ** -0.5),
        "g_post": 1.0 + nrm(ks[14], (DEPTH, D_MODEL), 0.05),
    }


def reference(x_prompt, x_sample, cache_a_k, cache_a_v, cache_b_k, cache_b_v, cache_b_logf,
              g_pre, w_in, b_f, rel_table, w_br_a, w_br_b, w_out, g_post):
    xp, xs = x_prompt, x_sample
    n_keep_p = min(BAND_PAST_CHUNKS * CHUNK, x_prompt.shape[1])
    n_cache_a = cache_a_k.shape[2]
    n_cache_b = cache_b_k.shape[2]
    akp, avp, bkp, bvp, blp = [], [], [], [], []
    aks, avs, bks, bvs, bls = [], [], [], [], []
    for l in range(DEPTH):
        qa, ka, va, za, qb, kb, vb, zb, logf, ma, mb = mixer_inputs(xp, g_pre[l], w_in[l], b_f[l])
        ya = chunk_band_attention_prompt(qa, ka, va, rel_table[l])
        yb = fox_prompt(qb, kb, vb, logf)
        akp.append(ka[:, -n_keep_p:])
        avp.append(va[:, -n_keep_p:])
        bkp.append(kb)
        bvp.append(vb)
        blp.append(logf)
        xp = mixer_outputs(xp, ya, yb, za, zb, ma, mb, w_br_a[l], w_br_b[l], w_out[l], g_post[l])

        qa, ka, va, za, qb, kb, vb, zb, logf, ma, mb = mixer_inputs(xs, g_pre[l], w_in[l], b_f[l])
        ka_all = jnp.concatenate([cache_a_k[l], ka], axis=1)
        va_all = jnp.concatenate([cache_a_v[l], va], axis=1)
        ya = chunk_band_attention_sample(qa, ka_all, va_all, n_cache_a, rel_table[l])
        kb_all = jnp.concatenate([cache_b_k[l], kb], axis=1)
        vb_all = jnp.concatenate([cache_b_v[l], vb], axis=1)
        lf_all = jnp.concatenate([cache_b_logf[l].astype(jnp.float32), logf], axis=1)
        yb = fox_sample(qb, kb_all, vb_all, lf_all, n_cache_b)
        aks.append(ka_all[:, -n_cache_a:])
        avs.append(va_all[:, -n_cache_a:])
        bks.append(kb)
        bvs.append(vb)
        bls.append(logf)
        xs = mixer_outputs(xs, ya, yb, za, zb, ma, mb, w_br_a[l], w_br_b[l], w_out[l], g_post[l])

    return (xp, xs,
            jnp.stack(akp), jnp.stack(avp), jnp.stack(bkp), jnp.stack(bvp), jnp.stack(blp),
            jnp.stack(aks), jnp.stack(avs), jnp.stack(bks), jnp.stack(bvs), jnp.stack(bls))
```

```python
import functools

import numpy as np
import jax
import jax.numpy as jnp
from jax import lax
from jax.experimental import pallas as pl
from jax.experimental.pallas import tpu as pltpu

D_MODEL = 1024
HEAD_DIM = 64
N_HEADS = 8
WIDTH = N_HEADS * HEAD_DIM
N_PAIRS = N_HEADS // 2
LANES = 128
CHUNK = 64
BAND_PAST_CHUNKS = 8
BAND_PAST = BAND_PAST_CHUNKS * CHUNK
REL_CLIP_PAST = 128
EPS = 1e-6
NEG_INF = -1e30
SCALE = HEAD_DIM ** -0.5
QBLK = 128
BAND_BLOCKS = BAND_PAST // QBLK + 1
VMEM_LIMIT = 56 * 1024 * 1024

F32 = jnp.float32
BF16 = jnp.bfloat16


def _dot(a, b):
    return jnp.dot(a, b, preferred_element_type=F32)


def _dot_nt(a, b):
    return lax.dot_general(a, b, (((1,), (1,)), ((), ())), preferred_element_type=F32)


def _split3(x):
    x1 = x.astype(BF16)
    r1 = x - x1.astype(F32)
    x2 = r1.astype(BF16)
    x3 = (r1 - x2.astype(F32)).astype(BF16)
    return x1, x2, x3


def _dot_f32_left(x, m):
    x1, x2, x3 = _split3(x)
    return _dot(x1, m) + _dot(x2, m) + _dot(x3, m)


def _dot_f32_right(m, x):
    x1, x2, x3 = _split3(x)
    return _dot(m, x1) + _dot(m, x2) + _dot(m, x3)


def _rmsnorm(x, g):
    ms = jnp.mean(x * x, axis=-1, keepdims=True)
    return x * lax.rsqrt(ms + EPS) * g


def _log_sigmoid(x):
    return jnp.minimum(x, 0.0) - jnp.log1p(jnp.exp(-jnp.abs(x)))


def _even_lane_mask(shape):
    return lax.broadcasted_iota(jnp.int32, shape, len(shape) - 1) < HEAD_DIM


def _qkv_kernel(x_ref, g_ref, w_ref, wf_ref, bf_ref,
                qa_ref, ka_ref, va_ref, qb_ref, kb_ref, vb_ref, kbh_ref, vbh_ref,
                lf_ref, kat_ref, vat_ref):
    h = _rmsnorm(x_ref[...], g_ref[...]).astype(BF16)

    def proj(c):
        return _dot(h, w_ref[:, c * WIDTH:(c + 1) * WIDTH])

    qa_ref[...] = (proj(0) * SCALE).astype(BF16)
    ka = proj(1)
    ka_ref[...] = ka.astype(BF16)
    kat_ref[...] = ka
    va = proj(2)
    va_ref[...] = va.astype(BF16)
    vat_ref[...] = va
    qb_ref[...] = (proj(3) * SCALE).astype(BF16)
    kb = proj(4)
    kb_ref[...] = kb
    kbh_ref[...] = kb.astype(BF16)
    vb = proj(5)
    vb_ref[...] = vb
    vbh_ref[...] = vb.astype(BF16)
    fl = _dot(h, wf_ref[...]) + bf_ref[...]
    lf_ref[...] = _log_sigmoid(fl)[:, :N_HEADS]


def _qkv_proj(x2, wt, *, tm, seg_rows, tail):
    rows = x2.shape[0]
    assert rows % tm == 0 and seg_rows % tm == 0 and tail == tm
    n_tiles = rows // tm
    tiles_per_seg = seg_rows // tm
    n_seg = rows // seg_rows
    row_blk = lambda i: (i, 0)
    const = lambda i: (0, 0)
    tail_blk = lambda i: (i // tiles_per_seg, 0)
    wide = pl.BlockSpec((tm, WIDTH), row_blk)
    bf16_out = jax.ShapeDtypeStruct((rows, WIDTH), BF16)
    f32_out = jax.ShapeDtypeStruct((rows, WIDTH), F32)
    tail_out = jax.ShapeDtypeStruct((n_seg * tail, WIDTH), F32)
    return pl.pallas_call(
        _qkv_kernel,
        grid=(n_tiles,),
        in_specs=[
            pl.BlockSpec((tm, D_MODEL), row_blk),
            pl.BlockSpec((1, D_MODEL), const),
            pl.BlockSpec((D_MODEL, 6 * WIDTH), const, pipeline_mode=pl.Buffered(1)),
            pl.BlockSpec((D_MODEL, LANES), const, pipeline_mode=pl.Buffered(1)),
            pl.BlockSpec((1, LANES), const),
        ],
        out_specs=[wide, wide, wide, wide, wide, wide, wide, wide,
                   pl.BlockSpec((tm, N_HEADS), row_blk),
                   pl.BlockSpec((tail, WIDTH), tail_blk),
                   pl.BlockSpec((tail, WIDTH), tail_blk)],
        out_shape=[bf16_out, bf16_out, bf16_out, bf16_out, f32_out, f32_out, bf16_out, bf16_out,
                   jax.ShapeDtypeStruct((rows, N_HEADS), F32), tail_out, tail_out],
        compiler_params=pltpu.CompilerParams(
            dimension_semantics=("arbitrary",), vmem_limit_bytes=VMEM_LIMIT),
        name="qkv_proj",
    )(x2, wt["g_pre"], wt["w_qkv"], wt["w_f"], wt["b_f"])


def _out_kernel(x_ref, ya_ref, yb_ref, g_ref, wg_ref, wa_ref, wb_ref, wo_ref, gp_ref, y_ref):
    x = x_ref[...]
    h = _rmsnorm(x, g_ref[...]).astype(BF16)
    za = _dot(h, wg_ref[:, 0:WIDTH])
    a = (ya_ref[...].astype(F32) * (za * jax.nn.sigmoid(za))).astype(BF16)
    ua = _dot(a, wa_ref[...])
    zb = _dot(h, wg_ref[:, WIDTH:2 * WIDTH])
    b = (yb_ref[...].astype(F32) * (zb * jax.nn.sigmoid(zb))).astype(BF16)
    ub = _dot(b, wb_ref[...])
    ma = _dot(h, wg_ref[:, 2 * WIDTH:2 * WIDTH + D_MODEL])
    mb = _dot(h, wg_ref[:, 2 * WIDTH + D_MODEL:2 * WIDTH + 2 * D_MODEL])
    merged = (jax.nn.sigmoid(ma) * ua + jax.nn.sigmoid(mb) * ub).astype(BF16)
    out = _dot(merged, wo_ref[...])
    y_ref[...] = x + _rmsnorm(out, gp_ref[...])


def _out_proj(x2, ya, yb, wt, *, tm):
    rows = x2.shape[0]
    assert rows % tm == 0
    row_blk = lambda i: (i, 0)
    const = lambda i: (0, 0)
    resident = functools.partial(pl.BlockSpec, index_map=const, pipeline_mode=pl.Buffered(1))
    return pl.pallas_call(
        _out_kernel,
        grid=(rows // tm,),
        in_specs=[
            pl.BlockSpec((tm, D_MODEL), row_blk),
            pl.BlockSpec((tm, WIDTH), row_blk),
            pl.BlockSpec((tm, WIDTH), row_blk),
            pl.BlockSpec((1, D_MODEL), const),
            resident((D_MODEL, 2 * WIDTH + 2 * D_MODEL)),
            resident((WIDTH, D_MODEL)),
            resident((WIDTH, D_MODEL)),
            resident((D_MODEL, D_MODEL)),
            pl.BlockSpec((1, D_MODEL), const),
        ],
        out_specs=pl.BlockSpec((tm, D_MODEL), row_blk),
        out_shape=jax.ShapeDtypeStruct((rows, D_MODEL), F32),
        compiler_params=pltpu.CompilerParams(
            dimension_semantics=("arbitrary",), vmem_limit_bytes=VMEM_LIMIT),
        name="out_proj",
    )(x2, ya, yb, wt["g_pre"], wt["w_gate"], wt["w_br_a"], wt["w_br_b"], wt["w_out"], wt["g_post"])


def _band_kernel(q_ref, k_ref, v_ref, bias_ref, o_ref):
    i = pl.program_id(2)
    q2 = q_ref[...]
    even = _even_lane_mask(q2.shape)
    zero = jnp.zeros_like(q2)
    q_e = jnp.where(even, q2, zero)
    q_o = jnp.where(even, zero, q2)
    s_e, s_o, v_blocks = [], [], []
    for d in range(BAND_BLOCKS):
        jb = i - (BAND_BLOCKS - 1) + d
        start = pl.multiple_of(jnp.maximum(jb, 0) * QBLK, QBLK)
        kd = k_ref[pl.ds(start, QBLK), :]
        v_blocks.append(v_ref[pl.ds(start, QBLK), :])
        penalty = jnp.where(jb >= 0, 0.0, NEG_INF).astype(F32)
        cols = slice(d * QBLK, (d + 1) * QBLK)
        s_e.append(_dot_nt(q_e, kd) + (bias_ref[0, :, cols] + penalty))
        s_o.append(_dot_nt(q_o, kd) + (bias_ref[1, :, cols] + penalty))

    def softmax_parts(s):
        m = functools.reduce(jnp.maximum, [jnp.max(t, axis=1, keepdims=True) for t in s])
        p = [jnp.exp(t - m) for t in s]
        l = functools.reduce(jnp.add, [jnp.sum(t, axis=1, keepdims=True) for t in p])
        return p, l

    p_e, l_e = softmax_parts(s_e)
    p_o, l_o = softmax_parts(s_o)
    acc = jnp.zeros((QBLK, LANES), F32)
    for d in range(BAND_BLOCKS):
        vd = v_blocks[d]
        zv = jnp.zeros_like(vd)
        v_cat = jnp.concatenate([jnp.where(even, vd, zv), jnp.where(even, zv, vd)], axis=0)
        p_cat = jnp.concatenate([p_e[d], p_o[d]], axis=1).astype(BF16)
        acc = acc + _dot(p_cat, v_cat)
    o_ref[...] = (acc / jnp.where(even, l_e, l_o)).astype(BF16)


def _band_attention(q, k, v, bias, batch, seq):
    nq = seq // QBLK
    k3 = k.reshape(batch, seq, WIDTH)
    v3 = v.reshape(batch, seq, WIDTH)
    q_blk = lambda b, p, i: (b * nq + i, p)
    kv_blk = lambda b, p, i: (b, 0, p)
    return pl.pallas_call(
        _band_kernel,
        grid=(batch, N_PAIRS, nq),
        in_specs=[
            pl.BlockSpec((QBLK, LANES), q_blk),
            pl.BlockSpec((None, seq, LANES), kv_blk),
            pl.BlockSpec((None, seq, LANES), kv_blk),
            pl.BlockSpec((2, QBLK, BAND_BLOCKS * QBLK), lambda b, p, i: (p, 0, 0)),
        ],
        out_specs=pl.BlockSpec((QBLK, LANES), q_blk),
        out_shape=jax.ShapeDtypeStruct((batch * seq, WIDTH), BF16),
        compiler_params=pltpu.CompilerParams(
            dimension_semantics=("arbitrary", "arbitrary", "arbitrary"),
            vmem_limit_bytes=VMEM_LIMIT),
        name="band_attention",
    )(q, k3, v3, bias)


def _band_bias_prompt(table):
    r = np.arange(QBLK)[:, None]
    kk = np.arange(BAND_BLOCKS * QBLK)[None, :]
    dist = r + BAND_PAST - kk
    q_chunk = r // CHUNK
    k_chunk = kk // CHUNK - BAND_PAST_CHUNKS
    allowed = (k_chunk <= q_chunk) & (k_chunk >= q_chunk - BAND_PAST_CHUNKS)
    idx = np.clip(dist, -(CHUNK - 1), REL_CLIP_PAST) + (CHUNK - 1)
    bias = jnp.take(table.astype(F32), jnp.asarray(idx), axis=1)
    return jnp.where(jnp.asarray(allowed)[None], bias, NEG_INF)


def _cumsum_kernel(x_ref, o_ref):
    x = x_ref[...]
    n_chunks = x.shape[0]
    r = lax.broadcasted_iota(jnp.int32, (LANES, LANES), 0)
    c = lax.broadcasted_iota(jnp.int32, (LANES, LANES), 1)
    upper = (r <= c).astype(BF16)
    local = _dot_f32_left(x, upper)
    totals = jnp.broadcast_to(local[:, LANES - 1:LANES], (n_chunks, LANES))
    rc = lax.broadcasted_iota(jnp.int32, (n_chunks, n_chunks), 0)
    cc = lax.broadcasted_iota(jnp.int32, (n_chunks, n_chunks), 1)
    strict_lower = (cc < rc).astype(BF16)
    o_ref[...] = local + _dot_f32_right(strict_lower, totals)


def _cumsum_lanes(x):
    n, c, _ = x.shape
    blk = pl.BlockSpec((None, c, LANES), lambda i: (i, 0, 0))
    return pl.pallas_call(
        _cumsum_kernel,
        grid=(n,),
        in_specs=[blk],
        out_specs=blk,
        out_shape=jax.ShapeDtypeStruct(x.shape, F32),
        compiler_params=pltpu.CompilerParams(dimension_semantics=("arbitrary",)),
        name="forget_cumsum",
    )(x)


def _fox_kernel(nblk_ref, q_ref, k_ref, v_ref, fq_ref, fk_ref, o_ref):
    b, p, i = pl.program_id(0), pl.program_id(1), pl.program_id(2)
    nq = pl.num_programs(2)
    q2 = q_ref[...]
    even = _even_lane_mask(q2.shape)
    zero = jnp.zeros_like(q2)
    q_e = jnp.where(even, q2, zero)
    q_o = jnp.where(even, zero, q2)
    fq = fq_ref[...]
    fq_e, fq_o = fq[:, 0:1], fq[:, 1:2]
    row = lax.broadcasted_iota(jnp.int32, (QBLK, QBLK), 0)
    col = lax.broadcasted_iota(jnp.int32, (QBLK, QBLK), 1)
    causal = col <= row

    def step(j, carry, diagonal):
        m_e, l_e, m_o, l_o, acc = carry
        start = pl.multiple_of(j * QBLK, QBLK)
        kd = k_ref[pl.ds(start, QBLK), :]
        vd = v_ref[pl.ds(start, QBLK), :]
        fk = fk_ref[j]
        s_e = _dot_nt(q_e, kd) + (fq_e - fk[0:1, :])
        s_o = _dot_nt(q_o, kd) + (fq_o - fk[1:2, :])
        if diagonal:
            s_e = jnp.where(causal, s_e, NEG_INF)
            s_o = jnp.where(causal, s_o, NEG_INF)
        mn_e = jnp.maximum(m_e, jnp.max(s_e, axis=1, keepdims=True))
        mn_o = jnp.maximum(m_o, jnp.max(s_o, axis=1, keepdims=True))
        a_e = jnp.exp(m_e - mn_e)
        a_o = jnp.exp(m_o - mn_o)
        p_e = jnp.exp(s_e - mn_e)
        p_o = jnp.exp(s_o - mn_o)
        l_e = a_e * l_e + jnp.sum(p_e, axis=1, keepdims=True)
        l_o = a_o * l_o + jnp.sum(p_o, axis=1, keepdims=True)
        zv = jnp.zeros_like(vd)
        v_cat = jnp.concatenate([jnp.where(even, vd, zv), jnp.where(even, zv, vd)], axis=0)
        p_cat = jnp.concatenate([p_e, p_o], axis=1).astype(BF16)
        acc = acc * jnp.where(even, a_e, a_o) + _dot(p_cat, v_cat)
        return mn_e, l_e, mn_o, l_o, acc

    col1 = jnp.full((QBLK, 1), NEG_INF, F32)
    zcol = jnp.zeros((QBLK, 1), F32)
    carry = step(i, (col1, zcol, col1, zcol, jnp.zeros((QBLK, LANES), F32)), True)
    n_extra = nblk_ref[(b * N_PAIRS + p) * nq + i] - 1
    carry = lax.fori_loop(0, n_extra, lambda t, c: step(i - 1 - t, c, False), carry)
    _, l_e, _, l_o, acc = carry
    o_ref[...] = (acc / jnp.where(even, l_e, l_o)).astype(BF16)


def _fox_attention(q, k, v, f_cum, nblk, batch, seq):
    nq = seq // QBLK
    k3 = k.reshape(batch, seq, WIDTH)
    v3 = v.reshape(batch, seq, WIDTH)
    f4 = f_cum.reshape(batch, N_PAIRS, 2, seq)
    fq = jnp.transpose(f4, (0, 1, 3, 2))
    fk = jnp.transpose(f4.reshape(batch, N_PAIRS, 2, nq, QBLK), (0, 1, 3, 2, 4))
    q_blk = lambda b, p, i, n: (b * nq + i, p)
    kv_blk = lambda b, p, i, n: (b, 0, p)
    grid_spec = pltpu.PrefetchScalarGridSpec(
        num_scalar_prefetch=1,
        grid=(batch, N_PAIRS, nq),
        in_specs=[
            pl.BlockSpec((QBLK, LANES), q_blk),
            pl.BlockSpec((None, seq, LANES), kv_blk),
            pl.BlockSpec((None, seq, LANES), kv_blk),
            pl.BlockSpec((None, None, QBLK, 2), lambda b, p, i, n: (b, p, i, 0)),
            pl.BlockSpec((None, None, nq, 2, QBLK), lambda b, p, i, n: (b, p, 0, 0, 0)),
        ],
        out_specs=pl.BlockSpec((QBLK, LANES), q_blk),
    )
    return pl.pallas_call(
        _fox_kernel,
        grid_spec=grid_spec,
        out_shape=jax.ShapeDtypeStruct((batch * seq, WIDTH), BF16),
        compiler_params=pltpu.CompilerParams(
            dimension_semantics=("arbitrary", "arbitrary", "arbitrary"),
            vmem_limit_bytes=VMEM_LIMIT),
        name="fox_attention",
    )(nblk, q, k3, v3, fq, fk)


def _band_sample_kernel(q_ref, kn_ref, vn_ref, knf_ref, vnf_ref, ck_ref, cv_ref, bc_ref, bn_ref,
                        o_ref, ok_ref, ov_ref):
    n_cache = ck_ref.shape[0]
    t_new = q_ref.shape[0]
    outs = []
    for h in range(N_HEADS):
        sl = slice(h * HEAD_DIM, (h + 1) * HEAD_DIM)
        qh = q_ref[:, sl]
        s_c = _dot_nt(qh, ck_ref[:, sl].astype(BF16)) + bc_ref[h]
        s_n = _dot_nt(qh, kn_ref[:, sl]) + bn_ref[h]
        m = jnp.maximum(jnp.max(s_c, axis=1, keepdims=True), jnp.max(s_n, axis=1, keepdims=True))
        p_c = jnp.exp(s_c - m)
        p_n = jnp.exp(s_n - m)
        l = jnp.sum(p_c, axis=1, keepdims=True) + jnp.sum(p_n, axis=1, keepdims=True)
        o = _dot(p_c.astype(BF16), cv_ref[:, sl].astype(BF16)) + _dot(p_n.astype(BF16), vn_ref[:, sl])
        outs.append(o / l)
    o_ref[...] = jnp.concatenate(outs, axis=1).astype(BF16)
    ok_ref[0:n_cache - t_new, :] = ck_ref[t_new:n_cache, :]
    ok_ref[n_cache - t_new:n_cache, :] = knf_ref[...]
    ov_ref[0:n_cache - t_new, :] = cv_ref[t_new:n_cache, :]
    ov_ref[n_cache - t_new:n_cache, :] = vnf_ref[...]


def _band_sample(q, kn, vn, knf, vnf, cache_k, cache_v, bias_c, bias_n, batch, t_new):
    n_cache = cache_k.shape[1]
    new_blk = pl.BlockSpec((t_new, WIDTH), lambda b: (b, 0))
    cache_blk = pl.BlockSpec((None, n_cache, WIDTH), lambda b: (b, 0, 0))
    whole = lambda shape: pl.BlockSpec(shape, lambda b: (0,) * len(shape))
    return pl.pallas_call(
        _band_sample_kernel,
        grid=(batch,),
        in_specs=[new_blk, new_blk, new_blk, new_blk, new_blk, cache_blk, cache_blk,
                  whole(bias_c.shape), whole(bias_n.shape)],
        out_specs=[new_blk, cache_blk, cache_blk],
        out_shape=[jax.ShapeDtypeStruct((batch * t_new, WIDTH), BF16),
                   jax.ShapeDtypeStruct(cache_k.shape, F32),
                   jax.ShapeDtypeStruct(cache_v.shape, F32)],
        compiler_params=pltpu.CompilerParams(
            dimension_semantics=("arbitrary",), vmem_limit_bytes=VMEM_LIMIT),
        name="band_sample",
    )(q, kn, vn, knf, vnf, cache_k, cache_v, bias_c, bias_n)


def _band_bias_sample(table, n_cache, t_new):
    q_pos = n_cache + np.arange(t_new)[:, None]
    k_pos = np.arange(n_cache + t_new)[None, :]
    idx = np.clip(q_pos - k_pos, -(CHUNK - 1), REL_CLIP_PAST) + (CHUNK - 1)
    bias = jnp.take(table.astype(F32), jnp.asarray(idx), axis=1)
    return bias[:, :, :n_cache], bias[:, :, n_cache:]


def _fox_sample_kernel(q_ref, kn_ref, vn_ref, ck_ref, cv_ref, fc_ref, lfn_ref, lfnt_ref, o_ref):
    past = ck_ref.shape[0]
    t_new = q_ref.shape[0]
    r = lax.broadcasted_iota(jnp.int32, (t_new, t_new), 0)
    c = lax.broadcasted_iota(jnp.int32, (t_new, t_new), 1)
    causal = c <= r
    f_new_col = _dot_f32_right(causal.astype(BF16), lfn_ref[...])
    f_new_row = _dot_f32_left(lfnt_ref[...], (r <= c).astype(BF16))
    outs = []
    for h in range(N_HEADS):
        sl = slice(h * HEAD_DIM, (h + 1) * HEAD_DIM)
        qh = q_ref[:, sl]
        f_cache = fc_ref[h:h + 1, :]
        f_tot = f_cache[:, past - 1:past]
        fq = f_tot + f_new_col[:, h:h + 1]
        s_c = _dot_nt(qh, ck_ref[:, sl].astype(BF16)) + (fq - f_cache)
        s_n = _dot_nt(qh, kn_ref[:, sl]) + (fq - (f_tot + f_new_row[h:h + 1, :]))
        s_n = jnp.where(causal, s_n, NEG_INF)
        m = jnp.maximum(jnp.max(s_c, axis=1, keepdims=True), jnp.max(s_n, axis=1, keepdims=True))
        p_c = jnp.exp(s_c - m)
        p_n = jnp.exp(s_n - m)
        l = jnp.sum(p_c, axis=1, keepdims=True) + jnp.sum(p_n, axis=1, keepdims=True)
        o = _dot(p_c.astype(BF16), cv_ref[:, sl].astype(BF16)) + _dot(p_n.astype(BF16), vn_ref[:, sl])
        outs.append(o / l)
    o_ref[...] = jnp.concatenate(outs, axis=1).astype(BF16)


def _fox_sample(q, kn, vn, cache_k, cache_v, f_cache, lfn, lfnt, batch, t_new):
    past = cache_k.shape[1]
    new_blk = pl.BlockSpec((t_new, WIDTH), lambda b: (b, 0))
    cache_blk = pl.BlockSpec((None, past, WIDTH), lambda b: (b, 0, 0))
    return pl.pallas_call(
        _fox_sample_kernel,
        grid=(batch,),
        in_specs=[new_blk, new_blk, new_blk, cache_blk, cache_blk,
                  pl.BlockSpec((None, N_HEADS, past), lambda b: (b, 0, 0)),
                  pl.BlockSpec((t_new, N_HEADS), lambda b: (b, 0)),
                  pl.BlockSpec((None, N_HEADS, t_new), lambda b: (b, 0, 0))],
        out_specs=new_blk,
        out_shape=jax.ShapeDtypeStruct((batch * t_new, WIDTH), BF16),
        compiler_params=pltpu.CompilerParams(
            dimension_semantics=("arbitrary",), vmem_limit_bytes=VMEM_LIMIT),
        name="fox_sample",
    )(q, kn, vn, cache_k, cache_v, f_cache, lfn, lfnt)


def _prep_weights(g_pre, w_in, b_f, rel_table, w_br_a, w_br_b, w_out, g_post):
    w = WIDTH
    cut = lambda a, n: w_in[:, a:a + n]
    fl0 = 8 * w
    w_qkv = jnp.concatenate([cut(0, w), cut(w, w), cut(2 * w, w),
                             cut(4 * w, w), cut(5 * w, w), cut(6 * w, w)], axis=1).astype(BF16)
    w_f = jnp.pad(cut(fl0, N_HEADS), ((0, 0), (0, LANES - N_HEADS))).astype(BF16)
    w_gate = jnp.concatenate([cut(3 * w, w), cut(7 * w, w),
                              cut(fl0 + N_HEADS, 2 * D_MODEL)], axis=1).astype(BF16)
    return dict(
        g_pre=g_pre.reshape(1, D_MODEL).astype(F32),
        g_post=g_post.reshape(1, D_MODEL).astype(F32),
        w_qkv=w_qkv, w_f=w_f, w_gate=w_gate,
        b_f=jnp.pad(b_f.astype(F32), (0, LANES - N_HEADS)).reshape(1, LANES),
        rel_table=rel_table,
        w_br_a=w_br_a.astype(BF16), w_br_b=w_br_b.astype(BF16), w_out=w_out.astype(BF16),
    )


def _heads(t, lead):
    return t.reshape(*lead, N_HEADS, HEAD_DIM)


def _prompt_layer(x, wt):
    batch, seq, _ = x.shape
    assert seq % 512 == 0
    x2 = x.reshape(batch * seq, D_MODEL)
    keep = min(BAND_PAST, seq)
    qa, ka, va, qb, kb, vb, kbh, vbh, lf, ka_tail, va_tail = _qkv_proj(
        x2, wt, tm=512, seg_rows=seq, tail=keep)
    ya = _band_attention(qa, ka, va, _band_bias_prompt(wt["rel_table"]), batch, seq)
    nq = seq // QBLK
    lf_t = jnp.transpose(lf.reshape(batch, seq, N_HEADS), (0, 2, 1))
    f_cum = _cumsum_lanes(lf_t.reshape(batch * N_HEADS, nq, LANES)).reshape(batch, N_HEADS, seq)
    nblk = jnp.tile(jnp.arange(1, nq + 1, dtype=jnp.int32), batch * N_PAIRS)
    yb = _fox_attention(qb, kbh, vbh, f_cum, nblk, batch, seq)
    y = _out_proj(x2, ya, yb, wt, tm=512)
    return (y.reshape(batch, seq, D_MODEL),
            _heads(ka_tail, (batch, keep)), _heads(va_tail, (batch, keep)),
            _heads(kb, (batch, seq)), _heads(vb, (batch, seq)),
            lf.reshape(batch, seq, N_HEADS))


def _sample_layer(x, cache_a_k, cache_a_v, cache_b_k, cache_b_v, cache_b_logf, wt):
    batch, t_new, _ = x.shape
    rows = batch * t_new
    n_cache_a = cache_a_k.shape[1]
    past = cache_b_k.shape[1]
    assert past % LANES == 0
    x2 = x.reshape(rows, D_MODEL)
    qa, ka, va, qb, kb, vb, kbh, vbh, lf, ka_f32, va_f32 = _qkv_proj(
        x2, wt, tm=rows, seg_rows=rows, tail=rows)
    bias_c, bias_n = _band_bias_sample(wt["rel_table"], n_cache_a, t_new)
    ya, ak, av = _band_sample(qa, ka, va, ka_f32, va_f32,
                              cache_a_k.reshape(batch, n_cache_a, WIDTH),
                              cache_a_v.reshape(batch, n_cache_a, WIDTH),
                              bias_c, bias_n, batch, t_new)
    lfc_t = jnp.transpose(cache_b_logf.astype(F32), (0, 2, 1))
    f_cache = _cumsum_lanes(lfc_t.reshape(batch * N_HEADS, past // LANES, LANES))
    f_cache = f_cache.reshape(batch, N_HEADS, past)
    lfn_t = jnp.transpose(lf.reshape(batch, t_new, N_HEADS), (0, 2, 1))
    yb = _fox_sample(qb, kbh, vbh, cache_b_k.reshape(batch, past, WIDTH),
                     cache_b_v.reshape(batch, past, WIDTH), f_cache, lf, lfn_t, batch, t_new)
    y = _out_proj(x2, ya, yb, wt, tm=rows)
    return (y.reshape(batch, t_new, D_MODEL),
            _heads(ak, (batch, n_cache_a)), _heads(av, (batch, n_cache_a)),
            _heads(kb, (batch, t_new)), _heads(vb, (batch, t_new)),
            lf.reshape(batch, t_new, N_HEADS))


def kernel(x_prompt, x_sample, cache_a_k, cache_a_v, cache_b_k, cache_b_v, cache_b_logf,
           g_pre, w_in, b_f, rel_table, w_br_a, w_br_b, w_out, g_post):
    depth = g_pre.shape[0]
    xp, xs = x_prompt, x_sample
    prompt_outs, sample_outs = [], []
    for l in range(depth):
        wt = _prep_weights(g_pre[l], w_in[l], b_f[l], rel_table[l],
                           w_br_a[l], w_br_b[l], w_out[l], g_post[l])
        xp, *p_out = _prompt_layer(xp, wt)
        xs, *s_out = _sample_layer(xs, cache_a_k[l], cache_a_v[l], cache_b_k[l], cache_b_v[l],
                                   cache_b_logf[l], wt)
        prompt_outs.append(p_out)
        sample_outs.append(s_out)
    stack = lambda outs, k: jnp.stack([o[k] for o in outs])
    return (xp, xs,
            *[stack(prompt_outs, k) for k in range(5)],
            *[stack(sample_outs, k) for k in range(5)])
```

```python
import functools

import numpy as np
import jax
import jax.numpy as jnp
from jax import lax
from jax.experimental import pallas as pl
from jax.experimental.pallas import tpu as pltpu

D_MODEL = 1024
HEAD_DIM = 64
N_HEADS = 8
WIDTH = N_HEADS * HEAD_DIM
N_PAIRS = N_HEADS // 2
LANES = 128
CHUNK = 64
BAND_PAST_CHUNKS = 8
BAND_PAST = BAND_PAST_CHUNKS * CHUNK
REL_CLIP_PAST = 128
REL_TABLE = REL_CLIP_PAST + CHUNK
EPS = 1e-6
NEG_INF = -1e30
SCALE = HEAD_DIM ** -0.5
QBLK = 128
BAND_BLOCKS = BAND_PAST // QBLK + 1
BAND_WINDOW = BAND_BLOCKS * QBLK
BAND_EXT = BAND_WINDOW + QBLK
BAND_SUB = 4
FOX_TQ = 256
FOX_WINDOW = 1280
FOX_SKIP_LOGIT_GAP = 104.0
FOX_NORM_SLACK = 1.02
VMEM_LIMIT = 56 * 1024 * 1024

F32 = jnp.float32
BF16 = jnp.bfloat16


def _dot(a, b):
    return jnp.dot(a, b, preferred_element_type=F32)


def _dot_nt(a, b):
    return lax.dot_general(a, b, (((1,), (1,)), ((), ())), preferred_element_type=F32)


def _split3(x):
    x1 = x.astype(BF16)
    r1 = x - x1.astype(F32)
    x2 = r1.astype(BF16)
    x3 = (r1 - x2.astype(F32)).astype(BF16)
    return x1, x2, x3


def _dot_f32_left(x, m):
    x1, x2, x3 = _split3(x)
    return _dot(x1, m) + _dot(x2, m) + _dot(x3, m)


def _dot_f32_right(m, x):
    x1, x2, x3 = _split3(x)
    return _dot(m, x1) + _dot(m, x2) + _dot(m, x3)


def _rmsnorm(x, g):
    ms = jnp.mean(x * x, axis=-1, keepdims=True)
    return x * lax.rsqrt(ms + EPS) * g


def _log_sigmoid(x):
    return jnp.minimum(x, 0.0) - jnp.log1p(jnp.exp(-jnp.abs(x)))


def _even_lane_mask(shape):
    return lax.broadcasted_iota(jnp.int32, shape, len(shape) - 1) < HEAD_DIM


def _split_heads(x2):
    even = _even_lane_mask(x2.shape)
    zero = jnp.zeros_like(x2)
    return jnp.where(even, x2, zero), jnp.where(even, zero, x2)


def _stack_heads(v2):
    v_e, v_o = _split_heads(v2)
    return jnp.concatenate([v_e, v_o], axis=0)


def _qkv_kernel(x_ref, g_ref, w_ref, wf_ref, bf_ref, e2_ref,
                qa_ref, ka_ref, va_ref, qb_ref, kb_ref, vb_ref, kbh_ref, vbh_ref,
                lf_ref, lft_ref, kat_ref, vat_ref, st_ref):
    h = _rmsnorm(x_ref[...], g_ref[...]).astype(BF16)

    def proj(c):
        return _dot(h, w_ref[:, c * WIDTH:(c + 1) * WIDTH])

    qa_ref[...] = (proj(0) * SCALE).astype(BF16)
    ka = proj(1)
    ka_ref[...] = ka.astype(BF16)
    kat_ref[...] = ka
    va = proj(2)
    va_ref[...] = va.astype(BF16)
    vat_ref[...] = va
    qb = proj(3)
    qb_ref[...] = (qb * SCALE).astype(BF16)
    kb = proj(4)
    kb_ref[...] = kb
    kbh_ref[...] = kb.astype(BF16)
    vb = proj(5)
    vb_ref[...] = vb
    vbh_ref[...] = vb.astype(BF16)
    lf = _log_sigmoid(_dot(h, wf_ref[...]) + bf_ref[...])
    lf_ref[...] = lf[:, :N_HEADS]
    lft_ref[...] = lf.T[:N_HEADS, :]
    sq = jnp.concatenate([(qb * qb).astype(BF16), (kb * kb).astype(BF16)], axis=1)
    n2 = _dot(sq, e2_ref[...])
    st_ref[...] = jnp.max(n2.reshape(n2.shape[0] // QBLK, QBLK, LANES), axis=1)


def _norm_selector():
    e2 = np.zeros((2 * WIDTH, LANES), np.float32)
    for h in range(N_HEADS):
        e2[h * HEAD_DIM:(h + 1) * HEAD_DIM, h] = 1.0
        e2[WIDTH + h * HEAD_DIM:WIDTH + (h + 1) * HEAD_DIM, N_HEADS + h] = 1.0
    return jnp.asarray(e2, BF16)


def _qkv_proj(x2, wt, *, tm, seg_rows, tail):
    rows = x2.shape[0]
    assert rows % tm == 0 and seg_rows % tm == 0 and tail == tm and tm % QBLK == 0
    n_tiles = rows // tm
    tiles_per_seg = seg_rows // tm
    n_seg = rows // seg_rows
    groups = tm // QBLK
    row_blk = lambda i: (i, 0)
    const = lambda i: (0, 0)
    tail_blk = lambda i: (i // tiles_per_seg, 0)
    resident = functools.partial(pl.BlockSpec, index_map=const, pipeline_mode=pl.Buffered(1))
    wide = pl.BlockSpec((tm, WIDTH), row_blk)
    bf16_out = jax.ShapeDtypeStruct((rows, WIDTH), BF16)
    f32_out = jax.ShapeDtypeStruct((rows, WIDTH), F32)
    tail_out = jax.ShapeDtypeStruct((n_seg * tail, WIDTH), F32)
    return pl.pallas_call(
        _qkv_kernel,
        grid=(n_tiles,),
        in_specs=[
            pl.BlockSpec((tm, D_MODEL), row_blk),
            pl.BlockSpec((1, D_MODEL), const),
            resident((D_MODEL, 6 * WIDTH)),
            resident((D_MODEL, LANES)),
            pl.BlockSpec((1, LANES), const),
            resident((2 * WIDTH, LANES)),
        ],
        out_specs=[wide, wide, wide, wide, wide, wide, wide, wide,
                   pl.BlockSpec((tm, N_HEADS), row_blk),
                   pl.BlockSpec((N_HEADS, tm), lambda i: (0, i)),
                   pl.BlockSpec((tail, WIDTH), tail_blk),
                   pl.BlockSpec((tail, WIDTH), tail_blk),
                   pl.BlockSpec((None, groups, LANES), lambda i: (i, 0, 0))],
        out_shape=[bf16_out, bf16_out, bf16_out, bf16_out, f32_out, f32_out, bf16_out, bf16_out,
                   jax.ShapeDtypeStruct((rows, N_HEADS), F32),
                   jax.ShapeDtypeStruct((N_HEADS, rows), F32),
                   tail_out, tail_out,
                   jax.ShapeDtypeStruct((n_tiles, groups, LANES), F32)],
        compiler_params=pltpu.CompilerParams(
            dimension_semantics=("arbitrary",), vmem_limit_bytes=VMEM_LIMIT),
        name="qkv_proj",
    )(x2, wt["g_pre"], wt["w_qkv"], wt["w_f"], wt["b_f"], _norm_selector())


def _out_kernel(x_ref, ya_ref, yb_ref, g_ref, wg_ref, wa_ref, wb_ref, wo_ref, gp_ref, y_ref):
    x = x_ref[...]
    h = _rmsnorm(x, g_ref[...]).astype(BF16)
    za = _dot(h, wg_ref[:, 0:WIDTH])
    a = (ya_ref[...].astype(F32) * (za * jax.nn.sigmoid(za))).astype(BF16)
    ua = _dot(a, wa_ref[...])
    zb = _dot(h, wg_ref[:, WIDTH:2 * WIDTH])
    b = (yb_ref[...].astype(F32) * (zb * jax.nn.sigmoid(zb))).astype(BF16)
    ub = _dot(b, wb_ref[...])
    ma = _dot(h, wg_ref[:, 2 * WIDTH:2 * WIDTH + D_MODEL])
    mb = _dot(h, wg_ref[:, 2 * WIDTH + D_MODEL:2 * WIDTH + 2 * D_MODEL])
    merged = (jax.nn.sigmoid(ma) * ua + jax.nn.sigmoid(mb) * ub).astype(BF16)
    out = _dot(merged, wo_ref[...])
    y_ref[...] = x + _rmsnorm(out, gp_ref[...])


def _out_proj(x2, ya, yb, wt, *, tm):
    rows = x2.shape[0]
    assert rows % tm == 0
    row_blk = lambda i: (i, 0)
    const = lambda i: (0, 0)
    resident = functools.partial(pl.BlockSpec, index_map=const, pipeline_mode=pl.Buffered(1))
    return pl.pallas_call(
        _out_kernel,
        grid=(rows // tm,),
        in_specs=[
            pl.BlockSpec((tm, D_MODEL), row_blk),
            pl.BlockSpec((tm, WIDTH), row_blk),
            pl.BlockSpec((tm, WIDTH), row_blk),
            pl.BlockSpec((1, D_MODEL), const),
            resident((D_MODEL, 2 * WIDTH + 2 * D_MODEL)),
            resident((WIDTH, D_MODEL)),
            resident((WIDTH, D_MODEL)),
            resident((D_MODEL, D_MODEL)),
            pl.BlockSpec((1, D_MODEL), const),
        ],
        out_specs=pl.BlockSpec((tm, D_MODEL), row_blk),
        out_shape=jax.ShapeDtypeStruct((rows, D_MODEL), F32),
        compiler_params=pltpu.CompilerParams(
            dimension_semantics=("arbitrary",), vmem_limit_bytes=VMEM_LIMIT),
        name="out_proj",
    )(x2, ya, yb, wt["g_pre"], wt["w_gate"], wt["w_br_a"], wt["w_br_b"], wt["w_out"], wt["g_post"])


def _band_bias_block(ext_row):
    x = jnp.broadcast_to(ext_row, (QBLK, BAND_EXT))
    skew = pltpu.roll(x, 0, 1, stride=1, stride_axis=0)[:, QBLK:]
    r = lax.broadcasted_iota(jnp.int32, (QBLK, BAND_WINDOW), 0)
    c = lax.broadcasted_iota(jnp.int32, (QBLK, BAND_WINDOW), 1)
    lo = jnp.where(r < CHUNK, 0, CHUNK)
    allowed = (c >= lo) & (c < lo + BAND_WINDOW - CHUNK)
    return jnp.where(allowed, skew, NEG_INF)


def _band_block(q2, k_ref, v_ref, bias_sc, first_key_block, valid):
    q_e, q_o = _split_heads(q2)
    s_e, s_o, v_blocks = [], [], []
    for d in range(BAND_BLOCKS):
        if not valid[d]:
            continue
        start = (first_key_block + d) * QBLK
        if not isinstance(start, int):
            start = pl.multiple_of(start, QBLK)
        kd = k_ref[pl.ds(start, QBLK), :]
        v_blocks.append(v_ref[pl.ds(start, QBLK), :])
        cols = slice(d * QBLK, (d + 1) * QBLK)
        s_e.append(_dot_nt(q_e, kd) + bias_sc[0, :, cols])
        s_o.append(_dot_nt(q_o, kd) + bias_sc[1, :, cols])

    def softmax_parts(s):
        m = functools.reduce(jnp.maximum, [jnp.max(t, axis=1, keepdims=True) for t in s])
        p = [jnp.exp(t - m) for t in s]
        l = functools.reduce(jnp.add, [jnp.sum(t, axis=1, keepdims=True) for t in p])
        return p, l

    p_e, l_e = softmax_parts(s_e)
    p_o, l_o = softmax_parts(s_o)
    p_cat = jnp.concatenate(p_e + p_o, axis=1).astype(BF16)
    v_all = jnp.concatenate(v_blocks, axis=0)
    acc = _dot(p_cat, _stack_heads(v_all))
    even = _even_lane_mask(acc.shape)
    return (acc / jnp.where(even, l_e, l_o)).astype(BF16)


def _band_kernel(ext_ref, q_ref, k_ref, v_ref, o_ref, bias_sc):
    g = pl.program_id(2)

    @pl.when(g == 0)
    def _():
        bias_sc[0] = _band_bias_block(ext_ref[0:1, :])
        bias_sc[1] = _band_bias_block(ext_ref[1:2, :])

    def run(first_step):
        for sub in range(BAND_SUB):
            rows = slice(sub * QBLK, (sub + 1) * QBLK)
            if first_step:
                valid = [sub - (BAND_BLOCKS - 1) + d >= 0 for d in range(BAND_BLOCKS)]
                first = sub - (BAND_BLOCKS - 1)
            else:
                valid = [True] * BAND_BLOCKS
                first = g * BAND_SUB + sub - (BAND_BLOCKS - 1)
            o_ref[rows, :] = _band_block(q_ref[rows, :], k_ref, v_ref, bias_sc, first, valid)

    @pl.when(g == 0)
    def _():
        run(True)

    @pl.when(g > 0)
    def _():
        run(False)


def _band_attention(q, k, v, ext, batch, seq):
    step_rows = BAND_SUB * QBLK
    assert seq % step_rows == 0 and BAND_SUB >= BAND_BLOCKS - 1
    ng = seq // step_rows
    k3 = k.reshape(batch, seq, WIDTH)
    v3 = v.reshape(batch, seq, WIDTH)
    q_blk = lambda b, p, g: (b * ng + g, p)
    kv_blk = lambda b, p, g: (b, 0, p)
    return pl.pallas_call(
        _band_kernel,
        grid=(batch, N_PAIRS, ng),
        in_specs=[
            pl.BlockSpec((None, 2, BAND_EXT), lambda b, p, g: (p, 0, 0)),
            pl.BlockSpec((step_rows, LANES), q_blk),
            pl.BlockSpec((None, seq, LANES), kv_blk),
            pl.BlockSpec((None, seq, LANES), kv_blk),
        ],
        out_specs=pl.BlockSpec((step_rows, LANES), q_blk),
        out_shape=jax.ShapeDtypeStruct((batch * seq, WIDTH), BF16),
        scratch_shapes=[pltpu.VMEM((2, QBLK, BAND_WINDOW), F32)],
        compiler_params=pltpu.CompilerParams(
            dimension_semantics=("arbitrary", "arbitrary", "arbitrary"),
            vmem_limit_bytes=VMEM_LIMIT),
        name="band_attention",
    )(ext, q, k3, v3)


def _band_bias_rows(table):
    t = table.astype(F32)
    far = jnp.broadcast_to(t[:, REL_TABLE - 1:], (N_HEADS, BAND_EXT - REL_TABLE - CHUNK))
    near = jnp.broadcast_to(t[:, :1], (N_HEADS, CHUNK))
    ext = jnp.concatenate([far, t[:, ::-1], near], axis=1)
    return ext.reshape(N_PAIRS, 2, BAND_EXT)


def _cumsum_kernel(x_ref, o_ref):
    x = x_ref[...]
    n_chunks = x.shape[0]
    r = lax.broadcasted_iota(jnp.int32, (LANES, LANES), 0)
    c = lax.broadcasted_iota(jnp.int32, (LANES, LANES), 1)
    upper = (r <= c).astype(BF16)
    local = _dot_f32_left(x, upper)
    totals = jnp.broadcast_to(local[:, LANES - 1:LANES], (n_chunks, LANES))
    rc = lax.broadcasted_iota(jnp.int32, (n_chunks, n_chunks), 0)
    cc = lax.broadcasted_iota(jnp.int32, (n_chunks, n_chunks), 1)
    strict_lower = (cc < rc).astype(BF16)
    o_ref[...] = local + _dot_f32_right(strict_lower, totals)


def _cumsum_lanes(x, n_outer, n_inner, in_index):
    c = x.shape[0] * x.shape[1] // (n_outer * n_inner)
    return pl.pallas_call(
        _cumsum_kernel,
        grid=(n_outer, n_inner),
        in_specs=[pl.BlockSpec((None, c, LANES), lambda o, i: (*in_index(o, i), 0))],
        out_specs=pl.BlockSpec((None, None, c, LANES), lambda o, i: (o, i, 0, 0)),
        out_shape=jax.ShapeDtypeStruct((n_outer, n_inner, c, LANES), F32),
        compiler_params=pltpu.CompilerParams(dimension_semantics=("arbitrary", "arbitrary")),
        name="forget_cumsum",
    )(x)


def _fox_kernel(nextra_ref, q_ref, k_ref, v_ref, f_ref, o_ref, *, window):
    b, p, i = pl.program_id(0), pl.program_id(1), pl.program_id(2)
    n_steps = pl.num_programs(2)
    tq = q_ref.shape[0]
    t0 = pl.multiple_of(i * tq, tq)
    q_e, q_o = _split_heads(q_ref[...])
    f_t0 = f_ref[:, pl.ds(t0, LANES)][:, 0:1]
    n_clamped = (window - tq) // tq

    def window_pass(start, mask_all):
        kw = k_ref[pl.ds(start, window), :]
        vw = v_ref[pl.ds(start, window), :]
        fk = f_ref[:, pl.ds(start, window)] - f_t0
        u_e = _dot_nt(q_e, kw) - fk[0:1, :]
        u_o = _dot_nt(q_o, kw) - fk[1:2, :]
        if mask_all:
            q_pos = t0 + lax.broadcasted_iota(jnp.int32, (tq, window), 0)
            k_pos = start + lax.broadcasted_iota(jnp.int32, (tq, window), 1)
            keep = k_pos <= q_pos
            u_e = jnp.where(keep, u_e, NEG_INF)
            u_o = jnp.where(keep, u_o, NEG_INF)
        else:
            keep = (lax.broadcasted_iota(jnp.int32, (tq, tq), 1)
                    <= lax.broadcasted_iota(jnp.int32, (tq, tq), 0))
            head = window - tq
            u_e = jnp.concatenate([u_e[:, :head], jnp.where(keep, u_e[:, head:], NEG_INF)], axis=1)
            u_o = jnp.concatenate([u_o[:, :head], jnp.where(keep, u_o[:, head:], NEG_INF)], axis=1)
        m_e = jnp.max(u_e, axis=1, keepdims=True)
        m_o = jnp.max(u_o, axis=1, keepdims=True)
        p_e = jnp.exp(u_e - m_e)
        p_o = jnp.exp(u_o - m_o)
        l_e = jnp.sum(p_e, axis=1, keepdims=True)
        l_o = jnp.sum(p_o, axis=1, keepdims=True)
        p_cat = jnp.concatenate([p_e, p_o], axis=1).astype(BF16)
        return m_e, l_e, m_o, l_o, _dot(p_cat, _stack_heads(vw))

    normal_start = pl.multiple_of(jnp.maximum(t0 + tq - window, 0), QBLK)
    carry = lax.cond(i < n_clamped,
                     lambda: window_pass(0, True),
                     lambda: window_pass(normal_start, False))
    even = _even_lane_mask((tq, LANES))

    def block_step(t, carry):
        m_e, l_e, m_o, l_o, acc = carry
        start = pl.multiple_of(normal_start - (t + 1) * QBLK, QBLK)
        kd = k_ref[pl.ds(start, QBLK), :]
        vd = v_ref[pl.ds(start, QBLK), :]
        fk = f_ref[:, pl.ds(start, QBLK)] - f_t0
        u_e = _dot_nt(q_e, kd) - fk[0:1, :]
        u_o = _dot_nt(q_o, kd) - fk[1:2, :]
        mn_e = jnp.maximum(m_e, jnp.max(u_e, axis=1, keepdims=True))
        mn_o = jnp.maximum(m_o, jnp.max(u_o, axis=1, keepdims=True))
        a_e = jnp.exp(m_e - mn_e)
        a_o = jnp.exp(m_o - mn_o)
        p_e = jnp.exp(u_e - mn_e)
        p_o = jnp.exp(u_o - mn_o)
        l_e = a_e * l_e + jnp.sum(p_e, axis=1, keepdims=True)
        l_o = a_o * l_o + jnp.sum(p_o, axis=1, keepdims=True)
        p_cat = jnp.concatenate([p_e, p_o], axis=1).astype(BF16)
        acc = acc * jnp.where(even, a_e, a_o) + _dot(p_cat, _stack_heads(vd))
        return mn_e, l_e, mn_o, l_o, acc

    n_extra = nextra_ref[(b * N_PAIRS + p) * n_steps + i]
    _, l_e, _, l_o, acc = lax.fori_loop(0, n_extra, block_step, carry)
    o_ref[...] = (acc / jnp.where(even, l_e, l_o)).astype(BF16)


def _fox_plan(f_cum, stats, seq, window):
    batch = f_cum.shape[0]
    n_i = seq // FOX_TQ
    qn2 = stats[:, :, 0:N_HEADS] * (SCALE * SCALE)
    kn2 = stats[:, :, N_HEADS:2 * N_HEADS]
    qn2_blk = jnp.max(qn2.reshape(batch, n_i, FOX_TQ // QBLK, N_HEADS), axis=2)
    kn2_max = jnp.max(kn2, axis=1)
    qk = FOX_NORM_SLACK * jnp.sqrt(qn2_blk * kn2_max[:, None, :])
    f_start = f_cum[:, :, ::FOX_TQ]
    f_end = f_cum[:, :, QBLK - 1::QBLK]
    limit = f_start + 2.0 * jnp.transpose(qk, (0, 2, 1)) + FOX_SKIP_LOGIT_GAP
    first_needed = jnp.sum(f_end[:, :, None, :] > limit[..., None], axis=-1)
    first_needed = jnp.min(first_needed.reshape(batch, N_PAIRS, 2, n_i), axis=2)
    window_first = np.maximum((np.arange(n_i) + 1) * FOX_TQ - window, 0) // QBLK
    n_extra = jnp.maximum(jnp.asarray(window_first, jnp.int32) - first_needed.astype(jnp.int32), 0)
    return n_extra.reshape(-1)


def _fox_attention(q, k, v, f_cum, stats, batch, seq):
    assert seq % FOX_TQ == 0
    window = min(FOX_WINDOW, seq)
    assert window % FOX_TQ == 0
    n_i = seq // FOX_TQ
    n_extra = _fox_plan(f_cum, stats, seq, window)
    k3 = k.reshape(batch, seq, WIDTH)
    v3 = v.reshape(batch, seq, WIDTH)
    f4 = f_cum.reshape(batch, N_PAIRS, 2, seq)
    q_blk = lambda b, p, i, n: (b * n_i + i, p)
    kv_blk = lambda b, p, i, n: (b, 0, p)
    grid_spec = pltpu.PrefetchScalarGridSpec(
        num_scalar_prefetch=1,
        grid=(batch, N_PAIRS, n_i),
        in_specs=[
            pl.BlockSpec((FOX_TQ, LANES), q_blk),
            pl.BlockSpec((None, seq, LANES), kv_blk),
            pl.BlockSpec((None, seq, LANES), kv_blk),
            pl.BlockSpec((None, None, 2, seq), lambda b, p, i, n: (b, p, 0, 0)),
        ],
        out_specs=pl.BlockSpec((FOX_TQ, LANES), q_blk),
    )
    return pl.pallas_call(
        functools.partial(_fox_kernel, window=window),
        grid_spec=grid_spec,
        out_shape=jax.ShapeDtypeStruct((batch * seq, WIDTH), BF16),
        compiler_params=pltpu.CompilerParams(
            dimension_semantics=("arbitrary", "arbitrary", "arbitrary"),
            vmem_limit_bytes=VMEM_LIMIT),
        name="fox_attention",
    )(n_extra, q, k3, v3, f4)


def _band_sample_kernel(q_ref, kn_ref, vn_ref, knf_ref, vnf_ref, ck_ref, cv_ref, bc_ref, bn_ref,
                        o_ref, ok_ref, ov_ref):
    n_cache = ck_ref.shape[0]
    t_new = q_ref.shape[0]
    outs = []
    for h in range(N_HEADS):
        sl = slice(h * HEAD_DIM, (h + 1) * HEAD_DIM)
        qh = q_ref[:, sl]
        s_c = _dot_nt(qh, ck_ref[:, sl].astype(BF16)) + bc_ref[h]
        s_n = _dot_nt(qh, kn_ref[:, sl]) + bn_ref[h]
        m = jnp.maximum(jnp.max(s_c, axis=1, keepdims=True), jnp.max(s_n, axis=1, keepdims=True))
        p_c = jnp.exp(s_c - m)
        p_n = jnp.exp(s_n - m)
        l = jnp.sum(p_c, axis=1, keepdims=True) + jnp.sum(p_n, axis=1, keepdims=True)
        o = _dot(p_c.astype(BF16), cv_ref[:, sl].astype(BF16)) + _dot(p_n.astype(BF16), vn_ref[:, sl])
        outs.append(o / l)
    o_ref[...] = jnp.concatenate(outs, axis=1).astype(BF16)
    ok_ref[0:n_cache - t_new, :] = ck_ref[t_new:n_cache, :]
    ok_ref[n_cache - t_new:n_cache, :] = knf_ref[...]
    ov_ref[0:n_cache - t_new, :] = cv_ref[t_new:n_cache, :]
    ov_ref[n_cache - t_new:n_cache, :] = vnf_ref[...]


def _band_sample(q, kn, vn, knf, vnf, cache_k, cache_v, bias_c, bias_n, batch, t_new):
    n_cache = cache_k.shape[1]
    new_blk = pl.BlockSpec((t_new, WIDTH), lambda b: (b, 0))
    cache_blk = pl.BlockSpec((None, n_cache, WIDTH), lambda b: (b, 0, 0))
    whole = lambda shape: pl.BlockSpec(shape, lambda b: (0,) * len(shape))
    return pl.pallas_call(
        _band_sample_kernel,
        grid=(batch,),
        in_specs=[new_blk, new_blk, new_blk, new_blk, new_blk, cache_blk, cache_blk,
                  whole(bias_c.shape), whole(bias_n.shape)],
        out_specs=[new_blk, cache_blk, cache_blk],
        out_shape=[jax.ShapeDtypeStruct((batch * t_new, WIDTH), BF16),
                   jax.ShapeDtypeStruct(cache_k.shape, F32),
                   jax.ShapeDtypeStruct(cache_v.shape, F32)],
        compiler_params=pltpu.CompilerParams(
            dimension_semantics=("arbitrary",), vmem_limit_bytes=VMEM_LIMIT),
        name="band_sample",
    )(q, kn, vn, knf, vnf, cache_k, cache_v, bias_c, bias_n)


def _band_bias_sample(table, n_cache, t_new):
    q_pos = n_cache + np.arange(t_new)[:, None]
    k_pos = np.arange(n_cache + t_new)[None, :]
    idx = np.clip(q_pos - k_pos, -(CHUNK - 1), REL_CLIP_PAST) + (CHUNK - 1)
    bias = jnp.take(table.astype(F32), jnp.asarray(idx), axis=1)
    return bias[:, :, :n_cache], bias[:, :, n_cache:]


def _fox_sample_kernel(q_ref, kn_ref, vn_ref, ck_ref, cv_ref, fc_ref, lfn_ref, lfnt_ref, o_ref):
    past = ck_ref.shape[0]
    t_new = q_ref.shape[0]
    r = lax.broadcasted_iota(jnp.int32, (t_new, t_new), 0)
    c = lax.broadcasted_iota(jnp.int32, (t_new, t_new), 1)
    causal = c <= r
    f_new_col = _dot_f32_right(causal.astype(BF16), lfn_ref[...])
    f_new_row = _dot_f32_left(lfnt_ref[...], (r <= c).astype(BF16))
    outs = []
    for h in range(N_HEADS):
        sl = slice(h * HEAD_DIM, (h + 1) * HEAD_DIM)
        qh = q_ref[:, sl]
        f_cache = fc_ref[h:h + 1, :]
        f_tot = f_cache[:, past - 1:past]
        fq = f_tot + f_new_col[:, h:h + 1]
        s_c = _dot_nt(qh, ck_ref[:, sl].astype(BF16)) + (fq - f_cache)
        s_n = _dot_nt(qh, kn_ref[:, sl]) + (fq - (f_tot + f_new_row[h:h + 1, :]))
        s_n = jnp.where(causal, s_n, NEG_INF)
        m = jnp.maximum(jnp.max(s_c, axis=1, keepdims=True), jnp.max(s_n, axis=1, keepdims=True))
        p_c = jnp.exp(s_c - m)
        p_n = jnp.exp(s_n - m)
        l = jnp.sum(p_c, axis=1, keepdims=True) + jnp.sum(p_n, axis=1, keepdims=True)
        o = _dot(p_c.astype(BF16), cv_ref[:, sl].astype(BF16)) + _dot(p_n.astype(BF16), vn_ref[:, sl])
        outs.append(o / l)
    o_ref[...] = jnp.concatenate(outs, axis=1).astype(BF16)


def _fox_sample(q, kn, vn, cache_k, cache_v, f_cache, lfn, lfnt, batch, t_new):
    past = cache_k.shape[1]
    new_blk = pl.BlockSpec((t_new, WIDTH), lambda b: (b, 0))
    cache_blk = pl.BlockSpec((None, past, WIDTH), lambda b: (b, 0, 0))
    return pl.pallas_call(
        _fox_sample_kernel,
        grid=(batch,),
        in_specs=[new_blk, new_blk, new_blk, cache_blk, cache_blk,
                  pl.BlockSpec((None, N_HEADS, past), lambda b: (b, 0, 0)),
                  pl.BlockSpec((t_new, N_HEADS), lambda b: (b, 0)),
                  pl.BlockSpec((None, N_HEADS, t_new), lambda b: (b, 0, 0))],
        out_specs=new_blk,
        out_shape=jax.ShapeDtypeStruct((batch * t_new, WIDTH), BF16),
        compiler_params=pltpu.CompilerParams(
            dimension_semantics=("arbitrary",), vmem_limit_bytes=VMEM_LIMIT),
        name="fox_sample",
    )(q, kn, vn, cache_k, cache_v, f_cache, lfn, lfnt)


def _prep_weights(g_pre, w_in, b_f, rel_table, w_br_a, w_br_b, w_out, g_post):
    w = WIDTH
    cut = lambda a, n: w_in[:, a:a + n]
    fl0 = 8 * w
    w_qkv = jnp.concatenate([cut(0, w), cut(w, w), cut(2 * w, w),
                             cut(4 * w, w), cut(5 * w, w), cut(6 * w, w)], axis=1).astype(BF16)
    w_f = jnp.pad(cut(fl0, N_HEADS), ((0, 0), (0, LANES - N_HEADS))).astype(BF16)
    w_gate = jnp.concatenate([cut(3 * w, w), cut(7 * w, w),
                              cut(fl0 + N_HEADS, 2 * D_MODEL)], axis=1).astype(BF16)
    return dict(
        g_pre=g_pre.reshape(1, D_MODEL).astype(F32),
        g_post=g_post.reshape(1, D_MODEL).astype(F32),
        w_qkv=w_qkv, w_f=w_f, w_gate=w_gate,
        b_f=jnp.pad(b_f.astype(F32), (0, LANES - N_HEADS)).reshape(1, LANES),
        rel_table=rel_table,
        w_br_a=w_br_a.astype(BF16), w_br_b=w_br_b.astype(BF16), w_out=w_out.astype(BF16),
    )


def _heads(t, lead):
    return t.reshape(*lead, N_HEADS, HEAD_DIM)


def _prompt_layer(x, wt):
    batch, seq, _ = x.shape
    assert seq % 512 == 0
    x2 = x.reshape(batch * seq, D_MODEL)
    keep = min(BAND_PAST, seq)
    nk = seq // QBLK
    qa, ka, va, qb, kb, vb, kbh, vbh, lf, lf_t, ka_tail, va_tail, stats = _qkv_proj(
        x2, wt, tm=512, seg_rows=seq, tail=keep)
    ya = _band_attention(qa, ka, va, _band_bias_rows(wt["rel_table"]), batch, seq)
    f_cum = _cumsum_lanes(lf_t.reshape(N_HEADS, batch * nk, LANES), batch, N_HEADS,
                          lambda b, h: (h, b))
    f_cum = f_cum.reshape(batch, N_HEADS, seq)
    yb = _fox_attention(qb, kbh, vbh, f_cum, stats.reshape(batch, nk, LANES), batch, seq)
    y = _out_proj(x2, ya, yb, wt, tm=512)
    return (y.reshape(batch, seq, D_MODEL),
            _heads(ka_tail, (batch, keep)), _heads(va_tail, (batch, keep)),
            _heads(kb, (batch, seq)), _heads(vb, (batch, seq)),
            lf.reshape(batch, seq, N_HEADS))


def _sample_layer(x, cache_a_k, cache_a_v, cache_b_k, cache_b_v, cache_b_logf, wt):
    batch, t_new, _ = x.shape
    rows = batch * t_new
    n_cache_a = cache_a_k.shape[1]
    past = cache_b_k.shape[1]
    assert past % LANES == 0 and rows % QBLK == 0
    x2 = x.reshape(rows, D_MODEL)
    qa, ka, va, qb, kb, vb, kbh, vbh, lf, lf_t, ka_f32, va_f32, _ = _qkv_proj(
        x2, wt, tm=rows, seg_rows=rows, tail=rows)
    bias_c, bias_n = _band_bias_sample(wt["rel_table"], n_cache_a, t_new)
    ya, ak, av = _band_sample(qa, ka, va, ka_f32, va_f32,
                              cache_a_k.reshape(batch, n_cache_a, WIDTH),
                              cache_a_v.reshape(batch, n_cache_a, WIDTH),
                              bias_c, bias_n, batch, t_new)
    lfc_t = jnp.transpose(cache_b_logf.astype(F32), (0, 2, 1))
    f_cache = _cumsum_lanes(lfc_t.reshape(batch, N_HEADS * (past // LANES), LANES), batch, N_HEADS,
                            lambda b, h: (b, h))
    f_cache = f_cache.reshape(batch, N_HEADS, past)
    lfn_t = jnp.transpose(lf_t.reshape(N_HEADS, batch, t_new), (1, 0, 2))
    yb = _fox_sample(qb, kbh, vbh, cache_b_k.reshape(batch, past, WIDTH),
                     cache_b_v.reshape(batch, past, WIDTH), f_cache, lf, lfn_t, batch, t_new)
    y = _out_proj(x2, ya, yb, wt, tm=rows)
    return (y.reshape(batch, t_new, D_MODEL),
            _heads(ak, (batch, n_cache_a)), _heads(av, (batch, n_cache_a)),
            _heads(kb, (batch, t_new)), _heads(vb, (batch, t_new)),
            lf.reshape(batch, t_new, N_HEADS))


def kernel(x_prompt, x_sample, cache_a_k, cache_a_v, cache_b_k, cache_b_v, cache_b_logf,
           g_pre, w_in, b_f, rel_table, w_br_a, w_br_b, w_out, g_post):
    depth = g_pre.shape[0]
    xp, xs = x_prompt, x_sample
    prompt_outs, sample_outs = [], []
    for l in range(depth):
        wt = _prep_weights(g_pre[l], w_in[l], b_f[l], rel_table[l],
                           w_br_a[l], w_br_b[l], w_out[l], g_post[l])
        xp, *p_out = _prompt_layer(xp, wt)
        xs, *s_out = _sample_layer(xs, cache_a_k[l], cache_a_v[l], cache_b_k[l], cache_b_v[l],
                                   cache_b_logf[l], wt)
        prompt_outs.append(p_out)
        sample_outs.append(s_out)
    stack = lambda outs, k: jnp.stack([o[k] for o in outs])
    return (xp, xs,
            *[stack(prompt_outs, k) for k in range(5)],
            *[stack(sample_outs, k) for k in range(5)])
```
